```python
import jax, jax.numpy as jnp
from jax import lax
import numpy as np

D_MODEL = 2048
BATCH = 2
SEQ = 4096
DEPTH = 2
DEC_BATCH = 32
DEC_SEQ = 1
PAST_LEN = 8192
PAGE_SIZE = 128

N_EVEN = (DEPTH + 1) // 2
N_ODD = DEPTH // 2
H_A = 8
D_A = D_MODEL // 2 // H_A
CHUNK = 128
H_B = 8
D_B = D_MODEL // 2 // H_B
Q_BLOCK = 128
SB_BIAS_INIT = -6.0
H_C = 4
DK_C = D_MODEL // 2 // H_C
DV_C = D_MODEL // H_C
GATE_RANK = 16
GATE_TAU = 16.0
GLA_CHUNK = 64
N_EXPERTS = 16
N_GROUPS = 4
EXP_PER_GROUP = N_EXPERTS // N_GROUPS
TOP_K = 2
D_FF = 512
EPS = 1e-6

kernel_name = 'hybrid_gmlp_stickbreak_gla_moe_step'


def rms_norm(x, g):
    xf = x.astype(jnp.float32)
    y = xf * lax.rsqrt(jnp.mean(xf * xf, axis=-1, keepdims=True) + EPS)
    return (y * g.astype(jnp.float32)).astype(x.dtype)


def ada_mod(c, w, b):
    a = jnp.dot(jax.nn.silu(c), w) + b
    return jnp.split(a[:, None, :], 6, axis=-1)


def modulate(x, g, shift, scale):
    return rms_norm(x, g) * (1 + scale) + shift


def chunk_spatial_gate(v, w_s, b_s):
    B, L = v.shape[:2]
    n_chunks = -(-L // CHUNK)
    pad = n_chunks * CHUNK - L
    vp = jnp.pad(v, ((0, 0), (0, pad), (0, 0), (0, 0))).reshape(B, n_chunks, CHUNK, H_A, D_A)
    w = jnp.tril(w_s)
    s = jnp.einsum('hij,bcjhd->bcihd', w, vp) + b_s.T[None, None, :, :, None]
    return s.reshape(B, n_chunks * CHUNK, H_A, D_A)[:, :L]


def stick_breaking_attention(q, k, v, sb_bias, q_offset):
    B, Lq, H, d = q.shape
    Lk = k.shape[1]
    qblk = min(Q_BLOCK, Lq)
    n_blocks = -(-Lq // qblk)
    pad = n_blocks * qblk - Lq
    qb = jnp.pad(q, ((0, 0), (0, pad), (0, 0), (0, 0))).reshape(B, n_blocks, qblk, H, d).transpose(1, 0, 2, 3, 4)
    k_pos = jnp.arange(Lk)
    scale = d ** -0.5
    bias = sb_bias.astype(jnp.float32)[None, :, None, None]

    def one_block(args):
        q_blk, blk = args
        q_pos = q_offset + blk * qblk + jnp.arange(qblk)
        z = jnp.einsum('bqhd,bkhd->bhqk', q_blk, k, preferred_element_type=jnp.float32) * scale + bias
        mask = k_pos[None, :] < q_pos[:, None]
        log_keep = jnp.where(mask, jax.nn.log_sigmoid(-z), 0.0)
        log_after = lax.cumsum(log_keep, axis=3, reverse=True) - log_keep
        a = jnp.where(mask, jnp.exp(jax.nn.log_sigmoid(z) + log_after), 0.0)
        return jnp.einsum('bhqk,bkhd->bqhd', a.astype(v.dtype), v)

    out = lax.map(one_block, (qb, jnp.arange(n_blocks)))
    return out.transpose(1, 0, 2, 3, 4).reshape(B, n_blocks * qblk, H, d)[:, :Lq]


def gla_chunked(q, k, v, log_a, s0):
    out_dtype = v.dtype
    B, L = q.shape[:2]
    C = min(GLA_CHUNK, L)
    n = -(-L // C)
    pad = n * C - L

    def prep(t):
        t = jnp.pad(t.astype(jnp.float32), ((0, 0), (0, pad), (0, 0), (0, 0)))
        return t.reshape(B, n, C, t.shape[2], t.shape[3]).transpose(1, 0, 3, 2, 4)

    qc, kc, vc, ac = prep(q), prep(k), prep(v), prep(log_a)
    causal = jnp.tril(jnp.ones((C, C), dtype=bool))[:, :, None]

    def step(s, inp):
        qi, ki, vi, ai = inp
        b = jnp.cumsum(ai, axis=2)
        inter = jnp.einsum('bhik,bhkv->bhiv', qi * jnp.exp(b), s)
        diff = b[:, :, :, None, :] - b[:, :, None, :, :]
        decay = jnp.where(causal, jnp.exp(jnp.where(causal, diff, 0.0)), 0.0)
        att = jnp.einsum('bhik,bhjk,bhijk->bhij', qi, ki, decay)
        intra = jnp.einsum('bhij,bhjv->bhiv', att, vi)
        b_last = b[:, :, -1, :]
        s_new = jnp.exp(b_last)[..., None] * s + jnp.einsum('bhjk,bhjv->bhkv', ki * jnp.exp(b_last[:, :, None, :] - b), vi)
        return s_new, inter + intra

    s_final, out = lax.scan(step, s0.astype(jnp.float32), (qc, kc, vc, ac))
    out = out.transpose(1, 0, 3, 2, 4).reshape(B, n * C, v.shape[2], v.shape[3])[:, :L]
    return out.astype(out_dtype), s_final.astype(s0.dtype)


def mixer_ab(h, w_in, w_out, g_v, w_s, b_s, g_q, g_k, sb_bias, past_k, past_v, q_offset):
    B, L, _ = h.shape
    wa, wb = H_A * D_A, H_B * D_B
    u_a, v_a, q, k, v_b = jnp.split(jnp.dot(h, w_in), [wa, 2 * wa, 2 * wa + wb, 2 * wa + 2 * wb], axis=-1)
    u_a = jax.nn.gelu(u_a).reshape(B, L, H_A, D_A)
    v_a = rms_norm(jax.nn.gelu(v_a).reshape(B, L, H_A, D_A), g_v)
    out_a = u_a * chunk_spatial_gate(v_a, w_s, b_s)
    q = rms_norm(q.reshape(B, L, H_B, D_B), g_q)
    k = rms_norm(k.reshape(B, L, H_B, D_B), g_k)
    v_b = v_b.reshape(B, L, H_B, D_B)
    k_all = k if past_k is None else jnp.concatenate([past_k, k], axis=1)
    v_all = v_b if past_v is None else jnp.concatenate([past_v, v_b], axis=1)
    out_b = stick_breaking_attention(q, k_all, v_all, sb_bias, q_offset)
    y = jnp.dot(jnp.concatenate([out_a.reshape(B, L, wa), out_b.reshape(B, L, wb)], axis=-1), w_out)
    return y, v_a, k, v_b


def mixer_c(h, w_in, w_a1, w_a2, b_a, g_o, w_out, s0):
    B, L, _ = h.shape
    wk, wv = H_C * DK_C, H_C * DV_C
    q, k, v, g = jnp.split(jnp.dot(h, w_in), [wk, 2 * wk, 2 * wk + wv], axis=-1)
    q = q.reshape(B, L, H_C, DK_C) * DK_C ** -0.5
    k = k.reshape(B, L, H_C, DK_C)
    v = v.reshape(B, L, H_C, DV_C)
    log_a = (jax.nn.log_sigmoid(jnp.dot(jnp.dot(h, w_a1), w_a2) + b_a) / GATE_TAU).reshape(B, L, H_C, DK_C)
    o, s = gla_chunked(q, k, v, log_a, s0)
    o = rms_norm(o, g_o) * jax.nn.silu(g.reshape(B, L, H_C, DV_C))
    return jnp.dot(o.reshape(B, L, wv), w_out), s


def moe_ffn(h, w_router, b_router, w_gate, w_up, w_down):
    B, L, D = h.shape
    t = h.reshape(B * L, D)
    scores = jax.nn.sigmoid(jnp.dot(t, w_router, preferred_element_type=jnp.float32))
    biased = scores + b_router.astype(jnp.float32)
    group_score = lax.top_k(biased.reshape(-1, N_GROUPS, EXP_PER_GROUP), TOP_K)[0].sum(-1)
    group_sel = jax.nn.one_hot(jnp.argmax(group_score, axis=-1), N_GROUPS, dtype=jnp.bool_)
    in_group = jnp.repeat(group_sel, EXP_PER_GROUP, axis=-1)
    _, idx = lax.top_k(jnp.where(in_group, biased, -jnp.inf), TOP_K)
    w_sel = jnp.take_along_axis(scores, idx, axis=-1)
    w_sel = w_sel / jnp.sum(w_sel, axis=-1, keepdims=True)
    gates = jnp.einsum('tk,tke->te', w_sel, jax.nn.one_hot(idx, N_EXPERTS, dtype=jnp.float32))
    hg = jnp.einsum('td,edf->tef', t, w_gate)
    hu = jnp.einsum('td,edf->tef', t, w_up)
    act = jax.nn.silu(hg) * hu * gates[:, :, None].astype(t.dtype)
    return jnp.einsum('tef,efd->td', act, w_down).reshape(B, L, D)


def run_trunk(x, c, past_k, past_v, gla_s0, q_offset, w_ada, b_ada, g_norm_mix, g_norm_ffn,
              w_in_ab, w_out_ab, g_v_a, w_spatial, b_spatial, g_q, g_k, b_sb,
              w_in_c, w_gate_a1, w_gate_a2, b_gate_a, g_out_c, w_out_c,
              w_router, b_router, w_exp_gate, w_exp_up, w_exp_down):
    L = x.shape[1]
    chunk_start = ((L - 1) // CHUNK) * CHUNK
    chunk_v, new_k, new_v, new_s = [], [], [], []
    for layer in range(DEPTH):
        i = layer // 2
        sh_m, sc_m, gt_m, sh_f, sc_f, gt_f = ada_mod(c, w_ada[layer], b_ada[layer])
        h = modulate(x, g_norm_mix[layer], sh_m, sc_m)
        if layer % 2 == 0:
            y, v_rows, k_rows, vb_rows = mixer_ab(
                h, w_in_ab[i], w_out_ab[i], g_v_a[i], w_spatial[i], b_spatial[i], g_q[i], g_k[i], b_sb[i],
                None if past_k is None else past_k[i], None if past_v is None else past_v[i], q_offset)
            chunk_v.append(v_rows[:, chunk_start:])
            new_k.append(k_rows)
            new_v.append(vb_rows)
        else:
            y, s = mixer_c(h, w_in_c[i], w_gate_a1[i], w_gate_a2[i], b_gate_a[i], g_out_c[i], w_out_c[i], gla_s0[i])
            new_s.append(s)
        x = x + gt_m * y
        h = modulate(x, g_norm_ffn[layer], sh_f, sc_f)
        x = x + gt_f * moe_ffn(h, w_router, b_router, w_exp_gate[layer], w_exp_up[layer], w_exp_down[layer])
    return x, jnp.stack(chunk_v), jnp.stack(new_k), jnp.stack(new_v), jnp.stack(new_s)


def setup_inputs(seed: int = 0) -> dict:
    key = jax.random.key(seed)
    ks = jax.random.split(key, 32)
    f32 = jnp.float32

    def nrm(i, shape, scale=1.0):
        return jax.random.normal(ks[i], shape, f32) * scale

    n_pages = PAST_LEN // PAGE_SIZE
    n_pool = (DEC_BATCH * n_pages * 5) // 4
    w_ab_in = 2 * H_A * D_A + 3 * H_B * D_B
    w_ab_out = H_A * D_A + H_B * D_B
    w_c_in = 2 * H_C * DK_C + 2 * H_C * DV_C
    page_table = jax.random.permutation(ks[0], n_pool)[:DEC_BATCH * n_pages].reshape(DEC_BATCH, n_pages).astype(jnp.int32)
    return {
        'x_prompt': nrm(1, (BATCH, SEQ, D_MODEL)),
        'x_sample': nrm(2, (DEC_BATCH, DEC_SEQ, D_MODEL)),
        'cache_k': nrm(3, (N_EVEN, n_pool, PAGE_SIZE, H_B, D_B)),
        'cache_v': nrm(4, (N_EVEN, n_pool, PAGE_SIZE, H_B, D_B)),
        'state_gla': nrm(5, (N_ODD, DEC_BATCH, H_C, DK_C, DV_C)),
        'page_table': page_table,
        'c_prompt': nrm(6, (BATCH, D_MODEL)),
        'c_sample': nrm(7, (DEC_BATCH, D_MODEL)),
        'w_ada': nrm(8, (DEPTH, D_MODEL, 6 * D_MODEL), 0.5 * D_MODEL ** -0.5),
        'b_ada': nrm(9, (DEPTH, 6 * D_MODEL), 0.02),
        'g_norm_mix': 1.0 + nrm(10, (DEPTH, D_MODEL), 0.05),
        'g_norm_ffn': 1.0 + nrm(11, (DEPTH, D_MODEL), 0.05),
        'w_in_ab': nrm(12, (N_EVEN, D_MODEL, w_ab_in), D_MODEL ** -0.5),
        'w_out_ab': nrm(13, (N_EVEN, w_ab_out, D_MODEL), w_ab_out ** -0.5),
        'g_v_a': 1.0 + nrm(14, (N_EVEN, D_A), 0.05),
        'w_spatial': nrm(15, (N_EVEN, H_A, CHUNK, CHUNK), CHUNK ** -0.5),
        'b_spatial': 1.0 + nrm(16, (N_EVEN, H_A, CHUNK), 0.1),
        'g_q': 1.0 + nrm(17, (N_EVEN, D_B), 0.05),
        'g_k': 1.0 + nrm(18, (N_EVEN, D_B), 0.05),
        'b_sb': SB_BIAS_INIT + nrm(30, (N_EVEN, H_B), 0.1),
        'w_in_c': nrm(19, (N_ODD, D_MODEL, w_c_in), D_MODEL ** -0.5),
        'w_gate_a1': nrm(20, (N_ODD, D_MODEL, GATE_RANK), D_MODEL ** -0.5),
        'w_gate_a2': nrm(21, (N_ODD, GATE_RANK, H_C * DK_C), GATE_RANK ** -0.5),
        'b_gate_a': nrm(22, (N_ODD, H_C * DK_C), 0.5) + 2.0,
        'g_out_c': 1.0 + nrm(23, (N_ODD, DV_C), 0.05),
        'w_out_c': nrm(24, (N_ODD, H_C * DV_C, D_MODEL), (H_C * DV_C) ** -0.5),
        'w_router': nrm(25, (D_MODEL, N_EXPERTS), D_MODEL ** -0.5),
        'b_router': nrm(26, (N_EXPERTS,), 0.01),
        'w_exp_gate': nrm(27, (DEPTH, N_EXPERTS, D_MODEL, D_FF), D_MODEL ** -0.5),
        'w_exp_up': nrm(28, (DEPTH, N_EXPERTS, D_MODEL, D_FF), D_MODEL ** -0.5),
        'w_exp_down': nrm(29, (DEPTH, N_EXPERTS, D_FF, D_MODEL), D_FF ** -0.5),
    }


def reference(x_prompt, x_sample, cache_k, cache_v, state_gla, page_table, c_prompt, c_sample,
              w_ada, b_ada, g_norm_mix, g_norm_ffn, w_in_ab, w_out_ab, g_v_a, w_spatial, b_spatial,
              g_q, g_k, b_sb, w_in_c, w_gate_a1, w_gate_a2, b_gate_a, g_out_c, w_out_c,
              w_router, b_router, w_exp_gate, w_exp_up, w_exp_down):
    b_p = x_prompt.shape[0]
    s0_prompt = [jnp.zeros((b_p, H_C, DK_C, DV_C), x_prompt.dtype) for _ in range(N_ODD)]
    y_prompt, chunk_v_prompt, k_prompt, v_prompt, state_gla_prompt = run_trunk(
        x_prompt, c_prompt, None, None, s0_prompt, 0, w_ada, b_ada, g_norm_mix, g_norm_ffn,
        w_in_ab, w_out_ab, g_v_a, w_spatial, b_spatial, g_q, g_k, b_sb,
        w_in_c, w_gate_a1, w_gate_a2, b_gate_a, g_out_c, w_out_c,
        w_router, b_router, w_exp_gate, w_exp_up, w_exp_down)
    b_s = x_sample.shape[0]
    past_len = page_table.shape[1] * cache_k.shape[2]
    past_k = [cache_k[i][page_table].reshape(b_s, past_len, H_B, D_B) for i in range(N_EVEN)]
    past_v = [cache_v[i][page_table].reshape(b_s, past_len, H_B, D_B) for i in range(N_EVEN)]
    s0_sample = [state_gla[i] for i in range(N_ODD)]
    y_sample, chunk_v_sample, k_sample, v_sample, state_gla_sample = run_trunk(
        x_sample, c_sample, past_k, past_v, s0_sample, past_len, w_ada, b_ada, g_norm_mix, g_norm_ffn,
        w_in_ab, w_out_ab, g_v_a, w_spatial, b_spatial, g_q, g_k, b_sb,
        w_in_c, w_gate_a1, w_gate_a2, b_gate_a, g_out_c, w_out_c,
        w_router, b_router, w_exp_gate, w_exp_up, w_exp_down)
    return (y_prompt, y_sample, chunk_v_prompt, chunk_v_sample, k_prompt, v_prompt, k_sample, v_sample, state_gla_prompt, state_gla_sample)
```

```python
import functools

import numpy as np
import jax
import jax.numpy as jnp
from jax import lax
from jax.experimental import pallas as pl
from jax.experimental.pallas import tpu as pltpu

D_MODEL = 2048
H_A = 8
D_A = 128
CHUNK = 128
H_B = 8
D_B = 128
H_C = 4
DK_C = 256
DV_C = 512
GATE_TAU = 16.0
N_EXPERTS = 16
N_GROUPS = 4
EXP_PER_GROUP = 4
D_FF = 512
EPS = 1e-6

_BF = jnp.bfloat16
_F32 = jnp.float32
_MIB = 1024 * 1024

GLA_CHUNK_ROWS = 64
GLA_LEVELS = (1, 2, 4, 8, 16, 32)
SB_BLOCK = 256
SBS_PAGES_PER_STEP = 8


def _cparams(n_axes, vmem_mib):
    return pltpu.CompilerParams(dimension_semantics=("arbitrary",) * n_axes,
                                vmem_limit_bytes=vmem_mib * _MIB)


def _dot(a, b):
    return jnp.dot(a, b, preferred_element_type=_F32)


def _dot_nt(a, b):
    return lax.dot_general(a, b, (((1,), (1,)), ((), ())), preferred_element_type=_F32)


def _dot_tn(a, b):
    return lax.dot_general(a, b, (((0,), (0,)), ((), ())), preferred_element_type=_F32)


def _split_dot(x, m01):
    hi = x.astype(_BF)
    lo = (x - hi.astype(_F32)).astype(_BF)
    return _dot(hi, m01) + _dot(lo, m01)


def _split_dot_left(m01, x):
    hi = x.astype(_BF)
    lo = (x - hi.astype(_F32)).astype(_BF)
    return _dot(m01, hi) + _dot(m01, lo)


def _rms(x, g):
    ms = jnp.mean(x * x, axis=-1, keepdims=True)
    return x * lax.rsqrt(ms + EPS) * g


def _log_sigmoid_pair(z):
    l = jnp.log1p(jnp.exp(-jnp.abs(z)))
    return jnp.minimum(z, 0.0) - l, jnp.minimum(-z, 0.0) - l


def _adaln_body(c_ref, w_ref, b_ref, o_ref):
    s = jax.nn.silu(c_ref[...]).astype(_BF)
    o_ref[...] = _dot(s, w_ref[...].astype(_BF)) + b_ref[...]


def _adaln(c_all, w_ada, b_ada):
    nl, d, n = w_ada.shape
    r = c_all.shape[0]
    tn = 1024
    return pl.pallas_call(
        _adaln_body,
        grid=(nl, n // tn),
        in_specs=[pl.BlockSpec((r, d), lambda l, j: (0, 0)),
                  pl.BlockSpec((None, d, tn), lambda l, j: (l, 0, j)),
                  pl.BlockSpec((None, 1, tn), lambda l, j: (l, 0, j))],
        out_specs=pl.BlockSpec((None, r, tn), lambda l, j: (l, 0, j)),
        out_shape=jax.ShapeDtypeStruct((nl, r, n), _F32),
        compiler_params=_cparams(2, 40),
        name="adaln",
    )(c_all, w_ada, b_ada.reshape(nl, 1, n))


def _mod_spec(rm, tr, d):
    if rm == 1:
        return pl.BlockSpec((None, 1, d), lambda g, i: (g, 0, 0))
    return pl.BlockSpec((None, tr, d), lambda g, i: (g, i, 0))


def _modulate_body(x_ref, g_ref, sh_ref, sc_ref, o_ref):
    y = _rms(x_ref[...], g_ref[...])
    o_ref[...] = (y * (1.0 + sc_ref[...]) + sh_ref[...]).astype(o_ref.dtype)


def _modulate(x, g, shift, scale, tr):
    gn, r, d = x.shape
    rm = shift.shape[1]
    return pl.pallas_call(
        _modulate_body,
        grid=(gn, r // tr),
        in_specs=[pl.BlockSpec((None, tr, d), lambda g_, i: (g_, i, 0)),
                  pl.BlockSpec((1, d), lambda g_, i: (0, 0)),
                  _mod_spec(rm, tr, d), _mod_spec(rm, tr, d)],
        out_specs=pl.BlockSpec((None, tr, d), lambda g_, i: (g_, i, 0)),
        out_shape=jax.ShapeDtypeStruct((gn, r, d), _BF),
        compiler_params=_cparams(2, 40),
        name="modulate",
    )(x, g.reshape(1, d), shift, scale)


def _route(scores, biased):
    rows = [biased[e:e + 1, :] for e in range(N_EXPERTS)]
    srow = [scores[e:e + 1, :] for e in range(N_EXPERTS)]
    gscore = []
    for g in range(N_GROUPS):
        r = rows[g * EXP_PER_GROUP:(g + 1) * EXP_PER_GROUP]
        best = None
        for i in range(EXP_PER_GROUP):
            for j in range(i + 1, EXP_PER_GROUP):
                s = r[i] + r[j]
                best = s if best is None else jnp.maximum(best, s)
        gscore.append(best)
    gbest = gscore[0]
    gsel = jnp.zeros_like(gbest, dtype=jnp.int32)
    for g in range(1, N_GROUPS):
        upd = gscore[g] > gbest
        gsel = jnp.where(upd, g, gsel)
        gbest = jnp.where(upd, gscore[g], gbest)
    neg = jnp.float32(-jnp.inf)
    masked = [jnp.where(gsel == (e // EXP_PER_GROUP), rows[e], neg) for e in range(N_EXPERTS)]

    def first_max(vals):
        m = vals[0]
        for v in vals[1:]:
            m = jnp.maximum(m, v)
        found = jnp.zeros(m.shape, dtype=jnp.bool_)
        hits = []
        for v in vals:
            hit = jnp.logical_and(v == m, jnp.logical_not(found))
            hits.append(hit)
            found = jnp.logical_or(found, hit)
        return hits

    is1 = first_max(masked)
    masked2 = [jnp.where(is1[e], neg, masked[e]) for e in range(N_EXPERTS)]
    is2 = first_max(masked2)
    s1 = sum(jnp.where(is1[e], srow[e], 0.0) for e in range(N_EXPERTS))
    s2 = sum(jnp.where(is2[e], srow[e], 0.0) for e in range(N_EXPERTS))
    den = s1 + s2
    w1 = s1 / den
    w2 = s2 / den
    return [jnp.where(is1[e], w1, 0.0) + jnp.where(is2[e], w2, 0.0) for e in range(N_EXPERTS)]


def _modulate_route_body(x_ref, g_ref, sh_ref, sc_ref, wr_ref, br_ref, h_ref, gt_ref):
    y = _rms(x_ref[...], g_ref[...])
    h = y * (1.0 + sc_ref[...]) + sh_ref[...]
    h_ref[...] = h.astype(h_ref.dtype)
    h_hi = h.astype(_BF)
    h_lo = (h - h_hi.astype(_F32)).astype(_BF)
    wr = wr_ref[...]
    w_hi = wr.astype(_BF)
    w_lo = (wr - w_hi.astype(_F32)).astype(_BF)
    a = _dot_nt(jnp.concatenate([w_hi, w_lo], axis=0), h_hi)
    b = _dot_nt(w_hi, h_lo)
    logits = a[:N_EXPERTS, :] + a[N_EXPERTS:, :] + b
    scores = jax.nn.sigmoid(logits)
    for e, row in enumerate(_route(scores, scores + br_ref[...])):
        gt_ref[e:e + 1, :] = row


def _modulate_route(x, g, shift, scale, w_router_t, b_router_col, tr):
    gn, r, d = x.shape
    rm = shift.shape[1]
    nt = r // tr
    return pl.pallas_call(
        _modulate_route_body,
        grid=(gn, nt),
        in_specs=[pl.BlockSpec((None, tr, d), lambda g_, i: (g_, i, 0)),
                  pl.BlockSpec((1, d), lambda g_, i: (0, 0)),
                  _mod_spec(rm, tr, d), _mod_spec(rm, tr, d),
                  pl.BlockSpec((N_EXPERTS, d), lambda g_, i: (0, 0)),
                  pl.BlockSpec((N_EXPERTS, 1), lambda g_, i: (0, 0))],
        out_specs=[pl.BlockSpec((None, tr, d), lambda g_, i: (g_, i, 0)),
                   pl.BlockSpec((None, N_EXPERTS, tr), lambda g_, i: (g_, 0, i))],
        out_shape=[jax.ShapeDtypeStruct((gn, r, d), _BF),
                   jax.ShapeDtypeStruct((gn, N_EXPERTS, r), _F32)],
        compiler_params=_cparams(2, 40),
        name="modulate_route",
    )(x, g.reshape(1, d), shift, scale, w_router_t, b_router_col)


def _mm_body(*refs, k_splits, epilogue, n_extra):
    n_lhs = len(k_splits)
    lhs = refs[:n_lhs]
    w_ref = refs[n_lhs]
    extra = refs[n_lhs + 1:n_lhs + 1 + n_extra]
    o_ref = refs[n_lhs + 1 + n_extra]
    wbf = refs[n_lhs + 2 + n_extra]

    @pl.when(pl.program_id(1) == 0)
    def _():
        wbf[...] = w_ref[...].astype(_BF)

    acc = None
    off = 0
    for lr, kk in zip(lhs, k_splits):
        part = _dot(lr[...], wbf[off:off + kk, :])
        acc = part if acc is None else acc + part
        off += kk
    o_ref[...] = epilogue(acc, *[e[...] for e in extra]).astype(o_ref.dtype)


def _matmul(lhs_list, w, col0, n_cols, epilogue, extras, out_dtype, tm, tn, name):
    m = lhs_list[0].shape[0]
    k_splits = tuple(a.shape[1] for a in lhs_list)
    ktot = sum(k_splits)
    jb0 = col0 // tn
    in_specs = [pl.BlockSpec((tm, kk), lambda j, i: (i, 0)) for kk in k_splits]
    in_specs.append(pl.BlockSpec((ktot, tn), lambda j, i: (0, j + jb0)))
    args = list(lhs_list) + [w]
    for arr, kind in extras:
        if kind == "head_gain":
            in_specs.append(pl.BlockSpec(arr.shape, lambda j, i: (0, 0)))
        elif kind == "tile":
            in_specs.append(pl.BlockSpec((tm, tn), lambda j, i: (i, j)))
        else:
            gn, rm, _ = arr.shape
            tiles_per_group = (m // gn) // tm
            if rm == 1:
                in_specs.append(pl.BlockSpec((None, 1, tn),
                                             lambda j, i, t=tiles_per_group: (i // t, 0, j)))
            else:
                in_specs.append(pl.BlockSpec((None, tm, tn),
                                             lambda j, i, t=tiles_per_group: (i // t, i % t, j)))
        args.append(arr)
    body = functools.partial(_mm_body, k_splits=k_splits, epilogue=epilogue, n_extra=len(extras))
    return pl.pallas_call(
        body,
        grid=(n_cols // tn, m // tm),
        in_specs=in_specs,
        out_specs=pl.BlockSpec((tm, tn), lambda j, i: (i, j)),
        out_shape=jax.ShapeDtypeStruct((m, n_cols), out_dtype),
        scratch_shapes=[pltpu.VMEM((ktot, tn), _BF)],
        compiler_params=_cparams(2, 48),
        name=name,
    )(*args)


def _ep_identity(acc):
    return acc


def _ep_gelu(acc):
    return jax.nn.gelu(acc)


def _group_norm(x, g):
    outs = []
    for c in range(x.shape[1] // 128):
        outs.append(_rms(x[:, c * 128:(c + 1) * 128], g))
    return jnp.concatenate(outs, axis=1)


def _ep_gelu_norm(acc, g):
    return _group_norm(jax.nn.gelu(acc), g)


def _ep_norm(acc, g):
    return _group_norm(acc, g)


def _ep_residual(acc, x, gate):
    return x + gate * acc


def _loga_body(h_ref, w1_ref, w2_ref, b_ref, o_ref):
    t1 = _dot(h_ref[...], w1_ref[...].astype(_BF)).astype(_BF)
    x = _dot(t1, w2_ref[...].astype(_BF)) + b_ref[...]
    ls, _ = _log_sigmoid_pair(x)
    o_ref[...] = ls / GATE_TAU


def _loga(h, w1p, w2p, b, tm):
    m, d = h.shape
    n = w2p.shape[1]
    return pl.pallas_call(
        _loga_body,
        grid=(m // tm,),
        in_specs=[pl.BlockSpec((tm, d), lambda i: (i, 0)),
                  pl.BlockSpec(w1p.shape, lambda i: (0, 0)),
                  pl.BlockSpec(w2p.shape, lambda i: (0, 0)),
                  pl.BlockSpec((1, n), lambda i: (0, 0))],
        out_specs=pl.BlockSpec((tm, n), lambda i: (i, 0)),
        out_shape=jax.ShapeDtypeStruct((m, n), _F32),
        compiler_params=_cparams(1, 32),
        name="gla_log_decay",
    )(h, w1p, w2p, b.reshape(1, n))


def _spatial_body(ua_ref, va_ref, w_ref, bt_ref, o_ref, *, n_chunks):
    ri = lax.broadcasted_iota(jnp.int32, (CHUNK, CHUNK), 0)
    ci = lax.broadcasted_iota(jnp.int32, (CHUNK, CHUNK), 1)
    for h in range(H_A):
        w = jnp.where(ri >= ci, w_ref[h], 0.0).astype(_BF)
        bcol = bt_ref[:, h:h + 1]
        cs = slice(h * D_A, (h + 1) * D_A)
        for n in range(n_chunks):
            rs = slice(n * CHUNK, (n + 1) * CHUNK)
            s = _dot(w, va_ref[rs, cs].astype(_BF)) + bcol
            o_ref[rs, cs] = (ua_ref[rs, cs] * s).astype(o_ref.dtype)


def _spatial(ua, va, w_s, b_s_t, tm):
    m, n = ua.shape
    body = functools.partial(_spatial_body, n_chunks=tm // CHUNK)
    return pl.pallas_call(
        body,
        grid=(m // tm,),
        in_specs=[pl.BlockSpec((tm, n), lambda i: (i, 0)),
                  pl.BlockSpec((tm, n), lambda i: (i, 0)),
                  pl.BlockSpec(w_s.shape, lambda i: (0, 0, 0)),
                  pl.BlockSpec(b_s_t.shape, lambda i: (0, 0))],
        out_specs=pl.BlockSpec((tm, n), lambda i: (i, 0)),
        out_shape=jax.ShapeDtypeStruct((m, n), _BF),
        compiler_params=_cparams(1, 32),
        name="spatial_gate",
    )(ua, va, w_s, b_s_t)


def _spatial_row0_body(ua_ref, va_ref, w_ref, b_ref, o_ref):
    w = w_ref[...].astype(_BF).astype(_F32)
    v = va_ref[...].astype(_BF).astype(_F32)
    o_ref[...] = (ua_ref[...] * (w * v + b_ref[...])).astype(o_ref.dtype)


def _spatial_row0(ua, va, w00, b0):
    return pl.pallas_call(
        _spatial_row0_body,
        out_shape=jax.ShapeDtypeStruct(ua.shape, _BF),
        name="spatial_gate_row0",
    )(ua, va, w00, b0)


def _suffix_matrix(n):
    j = lax.broadcasted_iota(jnp.int32, (n, n), 0)
    s = lax.broadcasted_iota(jnp.int32, (n, n), 1)
    return jnp.where(j > s, 1.0, 0.0).astype(_BF)


def _sb_body(bias_ref, q_ref, k_ref, v_ref, o_ref):
    t = SB_BLOCK
    h = pl.program_id(1)
    qi = pl.program_id(2)
    bias = bias_ref[h]
    scale = D_B ** -0.5
    q = q_ref[...].astype(_BF)
    u = _suffix_matrix(t)
    row = lax.broadcasted_iota(jnp.int32, (t, t), 0)
    col = lax.broadcasted_iota(jnp.int32, (t, t), 1)
    causal = col < row

    def block(kb, acc, run, masked):
        ks = pl.multiple_of(kb * t, t)
        k = k_ref[pl.ds(ks, t), :].astype(_BF)
        v = v_ref[pl.ds(ks, t), :].astype(_BF)
        z = _dot_nt(q, k) * scale + bias
        ls_pos, ls_neg = _log_sigmoid_pair(z)
        lk = jnp.where(causal, ls_neg, 0.0) if masked else ls_neg
        log_after = _split_dot(lk, u) + run
        a = jnp.exp(ls_pos + log_after)
        if masked:
            a = jnp.where(causal, a, 0.0)
        acc = acc + _dot(a.astype(_BF), v)
        run = run + jnp.sum(lk, axis=-1, keepdims=True)
        return acc, run

    acc0 = jnp.zeros((t, D_B), _F32)
    run0 = jnp.zeros((t, 1), _F32)
    acc, run = block(qi, acc0, run0, True)

    def body(step, carry):
        return block(qi - 1 - step, carry[0], carry[1], False)

    acc, run = lax.fori_loop(0, qi, body, (acc, run))
    o_ref[...] = acc.astype(o_ref.dtype)


def _sb_prompt(q, k, v, bias):
    b, l, _ = q.shape
    t = SB_BLOCK
    return pl.pallas_call(
        _sb_body,
        grid=(b, H_B, l // t),
        in_specs=[pl.BlockSpec(memory_space=pltpu.SMEM),
                  pl.BlockSpec((None, t, D_B), lambda b_, h, i: (b_, i, h)),
                  pl.BlockSpec((None, l, D_B), lambda b_, h, i: (b_, 0, h)),
                  pl.BlockSpec((None, l, D_B), lambda b_, h, i: (b_, 0, h))],
        out_specs=pl.BlockSpec((None, t, D_B), lambda b_, h, i: (b_, i, h)),
        out_shape=jax.ShapeDtypeStruct(q.shape, _BF),
        compiler_params=_cparams(3, 40),
        name="stickbreak_prompt",
    )(bias, q, k, v)


def _sbs_body(pt_ref, q_ref, bias_ref, *refs):
    del pt_ref
    p_n = SBS_PAGES_PER_STEP
    k_refs = refs[:p_n]
    v_refs = refs[p_n:2 * p_n]
    o_ref = refs[2 * p_n]
    acc_ref = refs[2 * p_n + 1]
    run_ref = refs[2 * p_n + 2]
    g = pl.program_id(1)
    page = k_refs[0].shape[0] // H_B
    width = H_B * D_B

    @pl.when(g == 0)
    def _():
        acc_ref[...] = jnp.zeros_like(acc_ref)
        run_ref[...] = jnp.zeros_like(run_ref)

    def by_key(ref):
        heads = [ref[pl.ds(h, page, stride=H_B), :] for h in range(H_B)]
        return jnp.concatenate(heads, axis=1).astype(_BF)

    hrow = lax.broadcasted_iota(jnp.int32, (H_B, width), 0)
    lane = lax.broadcasted_iota(jnp.int32, (H_B, width), 1)
    own = (lane // D_B) == hrow
    q = jnp.broadcast_to(q_ref[...], (H_B, width))
    qbd = jnp.where(own, q, 0.0).astype(_BF)
    bias = bias_ref[...]
    scale = D_B ** -0.5
    u = _suffix_matrix(page)
    acc = acc_ref[...]
    run = run_ref[...]
    for p in range(p_n):
        z = _dot_nt(qbd, by_key(k_refs[p])) * scale + bias
        ls_pos, lk = _log_sigmoid_pair(z)
        a = jnp.exp(ls_pos + _split_dot(lk, u) + run)
        full = _dot(a.astype(_BF), by_key(v_refs[p]))
        acc = acc + jnp.where(own, full, 0.0)
        run = run + jnp.sum(lk, axis=-1, keepdims=True)
    acc_ref[...] = acc
    run_ref[...] = run

    @pl.when(g == pl.num_programs(1) - 1)
    def _():
        o_ref[...] = jnp.sum(acc, axis=0, keepdims=True).astype(o_ref.dtype)


def _sb_paged(q, bias_col, cache_k, cache_v, page_table):
    b = q.shape[0]
    n_pages = page_table.shape[1]
    _, page_rows, d = cache_k.shape
    width = H_B * d
    p_n = SBS_PAGES_PER_STEP
    steps = n_pages // p_n

    def page_spec(p):
        return pl.BlockSpec((None, page_rows, d),
                            lambda b_, g, pt, p=p: (pt[b_, n_pages - 1 - (g * p_n + p)], 0, 0))

    in_specs = [pl.BlockSpec((None, 1, width), lambda b_, g, pt: (b_, 0, 0)),
                pl.BlockSpec((H_B, 1), lambda b_, g, pt: (0, 0))]
    in_specs += [page_spec(p) for p in range(p_n)] * 2
    return pl.pallas_call(
        _sbs_body,
        grid_spec=pltpu.PrefetchScalarGridSpec(
            num_scalar_prefetch=1,
            grid=(b, steps),
            in_specs=in_specs,
            out_specs=pl.BlockSpec((None, 1, width), lambda b_, g, pt: (b_, 0, 0)),
            scratch_shapes=[pltpu.VMEM((H_B, width), _F32), pltpu.VMEM((H_B, 1), _F32)]),
        out_shape=jax.ShapeDtypeStruct((b, 1, width), _BF),
        compiler_params=_cparams(2, 40),
        name="stickbreak_paged",
    )(page_table, q, bias_col, *([cache_k] * p_n), *([cache_v] * p_n))


def _gla_matrices(tl):
    i = np.arange(tl)[:, None]
    t = np.arange(tl)[None, :]
    mats = []
    for s in GLA_LEVELS:
        same = (i // s) == (t // s)
        mats.append(same & ((i // s) % 2 == 1) & (t <= i))
        mats.append(same & ((i // s) % 2 == 0) & (t > i))
    c = GLA_CHUNK_ROWS
    same = (i // c) == (t // c)
    mats.append(same & (t <= i))
    mats.append(same & (t > i))
    mats.append(same)
    return jnp.asarray(np.concatenate(mats, axis=0).astype(np.float32), dtype=_BF)


def _gla_body(q_ref, k_ref, v_ref, g_ref, la_ref, s0_ref, m_ref, go_ref, o_ref, s_ref, st_ref, *, tl):
    li = pl.program_id(2)

    @pl.when(li == 0)
    def _():
        st_ref[...] = s0_ref[...].T

    q = q_ref[...] * (DK_C ** -0.5)
    k = k_ref[...]
    v_bf = v_ref[...].astype(_BF)
    la = la_ref[...]
    la_hi = la.astype(_BF)
    la_lo = (la - la_hi.astype(_F32)).astype(_BF)

    def expo(idx):
        m = m_ref[idx * tl:(idx + 1) * tl, :]
        return _dot(m, la_hi) + _dot(m, la_lo)

    ri = lax.broadcasted_iota(jnp.int32, (tl, tl), 0)
    ci = lax.broadcasted_iota(jnp.int32, (tl, tl), 1)
    att = jnp.where(ri == ci, _dot_nt(q.astype(_BF), k.astype(_BF)), 0.0)
    for lvl, s in enumerate(GLA_LEVELS):
        qf = (q * jnp.exp(expo(2 * lvl))).astype(_BF)
        kf = (k * jnp.exp(expo(2 * lvl + 1))).astype(_BF)
        rb = ri >> lvl
        cb = ci >> lvl
        pair = jnp.logical_and(rb - cb == 1, (rb & 1) == 1)
        att = att + jnp.where(pair, _dot_nt(qf, kf), 0.0)
    o_intra = _dot(att.astype(_BF), v_bf)

    n_lvl = 2 * len(GLA_LEVELS)
    qe = (q * jnp.exp(expo(n_lvl))).astype(_BF)
    kd = (k * jnp.exp(expo(n_lvl + 1))).astype(_BF)
    dtot = jnp.exp(expo(n_lvl + 2))
    c = GLA_CHUNK_ROWS
    for n in range(tl // c):
        rs = slice(n * c, (n + 1) * c)
        st = st_ref[...]
        o = _dot_nt(qe[rs], st.astype(_BF)) + o_intra[rs]
        upd = _dot_tn(v_bf[rs], kd[rs])
        st_ref[...] = st * dtot[n * c:n * c + 1, :] + upd
        on = _rms(o, go_ref[...]) * jax.nn.silu(g_ref[rs, :])
        o_ref[rs, :] = on.astype(o_ref.dtype)

    @pl.when(li == pl.num_programs(2) - 1)
    def _():
        s_ref[...] = st_ref[...].T


def _gla(p, la, s0, g_o, tl):
    b, l, _ = p.shape
    mats = _gla_matrices(tl)
    kq = (H_C * DK_C) // DK_C
    kv = (2 * H_C * DK_C) // DV_C
    kg = kv + H_C
    body = functools.partial(_gla_body, tl=tl)
    return pl.pallas_call(
        body,
        grid=(b, H_C, l // tl),
        in_specs=[pl.BlockSpec((None, tl, DK_C), lambda b_, h, i: (b_, i, h)),
                  pl.BlockSpec((None, tl, DK_C), lambda b_, h, i: (b_, i, kq + h)),
                  pl.BlockSpec((None, tl, DV_C), lambda b_, h, i: (b_, i, kv + h)),
                  pl.BlockSpec((None, tl, DV_C), lambda b_, h, i: (b_, i, kg + h)),
                  pl.BlockSpec((None, tl, DK_C), lambda b_, h, i: (b_, i, h)),
                  pl.BlockSpec((None, None, DK_C, DV_C), lambda b_, h, i: (b_, h, 0, 0)),
                  pl.BlockSpec(mats.shape, lambda b_, h, i: (0, 0)),
                  pl.BlockSpec((1, DV_C), lambda b_, h, i: (0, 0))],
        out_specs=[pl.BlockSpec((None, tl, DV_C), lambda b_, h, i: (b_, i, h)),
                   pl.BlockSpec((None, None, DK_C, DV_C), lambda b_, h, i: (b_, h, 0, 0))],
        out_shape=[jax.ShapeDtypeStruct((b, l, H_C * DV_C), _BF),
                   jax.ShapeDtypeStruct(s0.shape, _F32)],
        scratch_shapes=[pltpu.VMEM((DV_C, DK_C), _F32)],
        compiler_params=_cparams(3, 40),
        name="gla",
    )(p, p, p, p, la, s0, mats, g_o.reshape(1, DV_C))


def _moe_body(h_ref, gt_ref, wg_ref, wu_ref, wd_ref, x_ref, gate_ref, o_ref):
    e = pl.program_id(1)

    @pl.when(e == 0)
    def _():
        o_ref[...] = jnp.zeros_like(o_ref)

    h = h_ref[...]
    hg = _dot(h, wg_ref[...])
    hu = _dot(h, wu_ref[...])
    gt = gt_ref[...]
    lane = lax.broadcasted_iota(jnp.int32, gt.shape, 1)
    gcol = jnp.sum(jnp.where(lane == e, gt, 0.0), axis=-1, keepdims=True)
    act = (jax.nn.silu(hg) * hu * gcol).astype(_BF)
    o_ref[...] += _dot(act, wd_ref[...])

    @pl.when(e == pl.num_programs(1) - 1)
    def _():
        o_ref[...] = x_ref[...] + gate_ref[...] * o_ref[...]


def _moe_dense(h, gates, w_gate, w_up, w_down, layer, x, gate, tm):
    m, d = h.shape
    gn, rm, _ = gate.shape
    tiles_per_group = (m // gn) // tm
    if rm == 1:
        gate_spec = pl.BlockSpec((None, 1, d), lambda i, e: (i // tiles_per_group, 0, 0))
    else:
        gate_spec = pl.BlockSpec((None, tm, d), lambda i, e: (i // tiles_per_group, i % tiles_per_group, 0))
    return pl.pallas_call(
        _moe_body,
        grid=(m // tm, N_EXPERTS),
        in_specs=[pl.BlockSpec((tm, d), lambda i, e: (i, 0)),
                  pl.BlockSpec((tm, N_EXPERTS), lambda i, e: (i, 0)),
                  pl.BlockSpec((None, None, d, D_FF), lambda i, e: (layer, e, 0, 0)),
                  pl.BlockSpec((None, None, d, D_FF), lambda i, e: (layer, e, 0, 0)),
                  pl.BlockSpec((None, None, D_FF, d), lambda i, e: (layer, e, 0, 0)),
                  pl.BlockSpec((tm, d), lambda i, e: (i, 0)),
                  gate_spec],
        out_specs=pl.BlockSpec((tm, d), lambda i, e: (i, 0)),
        out_shape=jax.ShapeDtypeStruct((m, d), _F32),
        compiler_params=_cparams(2, 48),
        name="moe_experts",
    )(h, gates, w_gate, w_up, w_down, x, gate)


def _trunk(x, mods, group_shape, past, gla_s0, params, tm):
    (g_norm_mix, g_norm_ffn, w_in_ab, w_out_ab, g_v_a, w_spatial, b_spatial, g_q, g_k, b_sb,
     w_in_c, w_gate_a1, w_gate_a2, b_gate_a, g_out_c, w_out_c, w_router, b_router,
     w_exp_gate, w_exp_up, w_exp_down) = params
    gn, r = group_shape
    m = gn * r
    tr = min(tm, r)
    wa = H_A * D_A
    wb = H_B * D_B
    w_router_t = w_router.T
    b_router_col = b_router.reshape(N_EXPERTS, 1)
    outs = {}
    for layer in range(2):
        sh_m, sc_m, gt_m, sh_f, sc_f, gt_f = mods[layer]
        h = _modulate(x.reshape(gn, r, D_MODEL), g_norm_mix[layer], sh_m, sc_m, tr).reshape(m, D_MODEL)
        if layer == 0:
            w_in = w_in_ab[0]
            mm = functools.partial(_matmul, [h], w_in, tm=tm, tn=1024)
            gain = lambda v: [(v.reshape(1, -1), "head_gain")]
            u_a = mm(0, wa, _ep_gelu, [], _F32, name="in_ab_u")
            v_a = mm(wa, wa, _ep_gelu_norm, gain(g_v_a[0]), _F32, name="in_ab_v")
            q = mm(2 * wa, wb, _ep_norm, gain(g_q[0]), _F32, name="in_ab_q")
            k = mm(2 * wa + wb, wb, _ep_norm, gain(g_k[0]), _F32, name="in_ab_k")
            v_b = mm(2 * wa + 2 * wb, wb, _ep_identity, [], _F32, name="in_ab_vb")
            if past is None:
                out_a = _spatial(u_a, v_a, w_spatial[0], b_spatial[0].T, tm)
                out_b = _sb_prompt(q.reshape(gn, r, wb), k.reshape(gn, r, wb), v_b.reshape(gn, r, wb),
                                   b_sb[0]).reshape(m, wb)
            else:
                w00 = jnp.repeat(w_spatial[0][:, 0, 0], D_A).reshape(1, wa)
                b0 = jnp.repeat(b_spatial[0][:, 0], D_A).reshape(1, wa)
                out_a = _spatial_row0(u_a, v_a, w00, b0)
                cache_k, cache_v, page_table = past
                out_b = _sb_paged(q.reshape(m, 1, wb), b_sb[0].reshape(H_B, 1), cache_k, cache_v,
                                  page_table).reshape(m, wb)
            outs["v_a"], outs["k"], outs["v_b"] = v_a, k, v_b
            x = _matmul([out_a, out_b], w_out_ab[0], 0, D_MODEL, _ep_residual,
                        [(x, "tile"), (gt_m, "group")], _F32, tm, 1024, "out_ab")
        else:
            p = _matmul([h], w_in_c[0], 0, w_in_c.shape[2], _ep_identity, [], _F32, tm, 1024, "in_c")
            w1p = jnp.pad(w_gate_a1[0], ((0, 0), (0, 128 - w_gate_a1.shape[2])))
            w2p = jnp.pad(w_gate_a2[0], ((0, 128 - w_gate_a2.shape[1]), (0, 0)))
            la = _loga(h, w1p, w2p, b_gate_a[0], tm)
            if r >= 256:
                tl = 256
                p3 = p.reshape(gn, r, -1)
                la3 = la.reshape(gn, r, -1)
            else:
                tl = GLA_CHUNK_ROWS
                p3 = jnp.pad(p.reshape(m, 1, -1), ((0, 0), (0, tl - 1), (0, 0)))
                la3 = jnp.pad(la.reshape(m, 1, -1), ((0, 0), (0, tl - 1), (0, 0)))
            o, s_new = _gla(p3, la3, gla_s0, g_out_c[0], tl)
            o = o.reshape(gn, r, -1).reshape(m, -1) if r >= 256 else o[:, 0, :]
            outs["s"] = s_new
            x = _matmul([o], w_out_c[0], 0, D_MODEL, _ep_residual,
                        [(x, "tile"), (gt_m, "group")], _F32, tm, 1024, "out_c")
        h, gates_t = _modulate_route(x.reshape(gn, r, D_MODEL), g_norm_ffn[layer], sh_f, sc_f,
                                     w_router_t, b_router_col, tr)
        gates = jnp.swapaxes(gates_t, 1, 2).reshape(m, N_EXPERTS)
        x = _moe_dense(h.reshape(m, D_MODEL), gates, w_exp_gate, w_exp_up, w_exp_down, layer, x, gt_f, tm)
    return x, outs


def kernel(x_prompt, x_sample, cache_k, cache_v, state_gla, page_table, c_prompt, c_sample, w_ada, b_ada, g_norm_mix, g_norm_ffn, w_in_ab, w_out_ab, g_v_a, w_spatial, b_spatial, g_q, g_k, b_sb, w_in_c, w_gate_a1, w_gate_a2, b_gate_a, g_out_c, w_out_c, w_router, b_router, w_exp_gate, w_exp_up, w_exp_down):
    bp, lp, d = x_prompt.shape
    bs = x_sample.shape[0]
    params = (g_norm_mix, g_norm_ffn, w_in_ab, w_out_ab, g_v_a, w_spatial, b_spatial, g_q, g_k, b_sb,
              w_in_c, w_gate_a1, w_gate_a2, b_gate_a, g_out_c, w_out_c, w_router, b_router,
              w_exp_gate.astype(_BF), w_exp_up.astype(_BF), w_exp_down.astype(_BF))

    ada = _adaln(jnp.concatenate([c_prompt, c_sample], axis=0), w_ada, b_ada)
    mods_p, mods_s = [], []
    for layer in range(ada.shape[0]):
        six = [ada[layer, :, i * d:(i + 1) * d] for i in range(6)]
        mods_p.append([a[:bp].reshape(bp, 1, d) for a in six])
        mods_s.append([a[bp:].reshape(1, bs, d) for a in six])

    s0_p = jnp.zeros((bp, H_C, DK_C, DV_C), _F32)
    y_p, o_p = _trunk(x_prompt.reshape(bp * lp, d), mods_p, (bp, lp), None, s0_p, params, 512)
    pool = cache_k.shape[1]
    past = (cache_k[0].reshape(pool, cache_k.shape[2] * H_B, D_B),
            cache_v[0].reshape(pool, cache_v.shape[2] * H_B, D_B), page_table)
    y_s, o_s = _trunk(x_sample.reshape(bs, d), mods_s, (1, bs), past, state_gla[0], params, bs)

    chunk_start = ((lp - 1) // CHUNK) * CHUNK
    va_p = o_p["v_a"].reshape(bp, lp, H_A, D_A)
    return (y_p.reshape(bp, lp, d),
            y_s.reshape(bs, 1, d),
            va_p[:, chunk_start:][None],
            o_s["v_a"].reshape(1, bs, 1, H_A, D_A),
            o_p["k"].reshape(1, bp, lp, H_B, D_B),
            o_p["v_b"].reshape(1, bp, lp, H_B, D_B),
            o_s["k"].reshape(1, bs, 1, H_B, D_B),
            o_s["v_b"].reshape(1, bs, 1, H_B, D_B),
            o_p["s"][None],
            o_s["s"][None])
```
